```python
import math
import functools
import jax
import jax.numpy as jnp
from jax import lax
import numpy as np

D_MODEL = 4096
BATCH = 4
SEQ = 2048
DEPTH = 2
DEC_BATCH = 8
DEC_SEQ = 1
PAST_LEN = 16384
PAGE_SIZE = 128

HEAD_DIM = 128
A_WIDTH = D_MODEL // 2
A_HEADS = A_WIDTH // HEAD_DIM
A_KV_HEADS = A_HEADS // 4
KV_WIDTH = A_KV_HEADS * HEAD_DIM
IDX_HEADS = 16
IDX_DIM = 64
TOPK_MAX = 256
ROPE_THETA = 500000.0
ROPE_DIV = 4
Q_BLOCK = 128

B_WIDTH = D_MODEL // 4
B_DK = 128
B_DV = 128
B_HEADS = B_WIDTH // B_DV
CONV_W = 4
DELTA_CHUNK = 64

C_WIDTH = D_MODEL - A_WIDTH - B_WIDTH
C_GROUP = 16
C_GROUPS = C_WIDTH // C_GROUP
C_STATE = 64

IN_WIDTH = A_WIDTH + 2 * KV_WIDTH + IDX_HEADS * IDX_DIM + IDX_DIM + IDX_HEADS + 4 * B_WIDTH + 2 * B_HEADS + C_WIDTH

N_EXPERTS = 16
N_EXPERT_GROUPS = 4
EXPERTS_PER_GROUP = N_EXPERTS // N_EXPERT_GROUPS
TOP_K = 2
D_EXPERT = D_MODEL // 4

DEEPNORM_ALPHA = (2 * DEPTH) ** 0.25
DEEPNORM_BETA = (8 * DEPTH) ** -0.25
LN_EPS = 1e-5

kernel_name = 'hybrid_dsa_deltanet_s5_moe_step'


def in_splits():
    sizes = (A_WIDTH, KV_WIDTH, KV_WIDTH, IDX_HEADS * IDX_DIM, IDX_DIM, IDX_HEADS,
             3 * B_WIDTH, B_WIDTH, B_HEADS, B_HEADS, C_WIDTH)
    return [int(o) for o in np.cumsum(sizes)[:-1]]


def layer_norm(x, g, b):
    xf = x.astype(jnp.float32)
    mu = jnp.mean(xf, axis=-1, keepdims=True)
    var = jnp.mean(jnp.square(xf - mu), axis=-1, keepdims=True)
    return ((xf - mu) * lax.rsqrt(var + LN_EPS) * g + b).astype(x.dtype)


def l2norm(x):
    return x * lax.rsqrt(jnp.sum(x * x, axis=-1, keepdims=True) + 1e-6)


def partial_rope(x, pos):
    rot = x.shape[-1] // ROPE_DIV
    half = rot // 2
    inv_freq = ROPE_THETA ** (-jnp.arange(half, dtype=jnp.float32) * 2.0 / rot)
    ang = pos.astype(jnp.float32)[:, None] * inv_freq[None, :]
    cos = jnp.cos(ang)[:, None, :]
    sin = jnp.sin(ang)[:, None, :]
    xf = x.astype(jnp.float32)
    x1, x2, rest = xf[..., :half], xf[..., half:rot], xf[..., rot:]
    out = jnp.concatenate([x1 * cos - x2 * sin, x2 * cos + x1 * sin, rest], axis=-1)
    return out.astype(x.dtype)


def indexer_scores(q_idx, w_idx, k_idx):
    dots = jnp.einsum('bthd,bld->bthl', q_idx, k_idx)
    return jnp.einsum('bthl,bth->btl', jax.nn.relu(dots), w_idx).astype(jnp.float32)


def select_keys(scores, q_pos):
    n_keys = scores.shape[-1]
    k = min(TOPK_MAX, n_keys // 4)
    key_pos = jnp.arange(n_keys, dtype=jnp.int32)
    admissible = key_pos[None, :] <= q_pos[:, None]
    masked = jnp.where(admissible[None], scores, -jnp.inf)
    _, idx = lax.top_k(masked, k)
    valid = idx <= q_pos[None, :, None]
    return idx, valid


def gather_rows(rows, idx):
    return jax.vmap(lambda r, i: r[i])(rows, idx)


def sparse_attend(q, k_sel, v_sel, valid):
    b, t, h, d = q.shape
    kvh = k_sel.shape[3]
    qg = q.reshape(b, t, kvh, h // kvh, d)
    s = jnp.einsum('btgrd,btkgd->btgrk', qg, k_sel).astype(jnp.float32) * (d ** -0.5)
    s = jnp.where(valid[:, :, None, None, :], s, -jnp.inf)
    p = jax.nn.softmax(s, axis=-1).astype(v_sel.dtype)
    o = jnp.einsum('btgrk,btkgd->btgrd', p, v_sel)
    return o.reshape(b, t, h * d)


def dsa_prompt(q, k, v, q_idx, w_idx, k_idx):
    b, s = q.shape[:2]
    n_blocks = s // Q_BLOCK

    def block(i):
        start = i * Q_BLOCK
        qb = lax.dynamic_slice_in_dim(q, start, Q_BLOCK, axis=1)
        qib = lax.dynamic_slice_in_dim(q_idx, start, Q_BLOCK, axis=1)
        wb = lax.dynamic_slice_in_dim(w_idx, start, Q_BLOCK, axis=1)
        pos = start + jnp.arange(Q_BLOCK, dtype=jnp.int32)
        idx, valid = select_keys(indexer_scores(qib, wb, k_idx), pos)
        return sparse_attend(qb, gather_rows(k, idx), gather_rows(v, idx), valid)

    out = lax.map(block, jnp.arange(n_blocks, dtype=jnp.int32))
    return jnp.transpose(out, (1, 0, 2, 3)).reshape(b, s, -1)


def dsa_sample(q, k_new, v_new, q_idx, w_idx, k_idx_new, pool_k, pool_v, pool_idx, page_table):
    db, ds = q.shape[:2]
    past = page_table.shape[1] * PAGE_SIZE
    k_idx_past = pool_idx[page_table].reshape(db, past, IDX_DIM)
    k_idx_all = jnp.concatenate([k_idx_past, k_idx_new.astype(k_idx_past.dtype)], axis=1)
    pos = past + jnp.arange(ds, dtype=jnp.int32)
    idx, valid = select_keys(indexer_scores(q_idx, w_idx, k_idx_all), pos)
    in_past = idx < past
    pidx = jnp.minimum(idx, past - 1)
    page_of = jnp.take_along_axis(page_table, (pidx // PAGE_SIZE).reshape(db, -1), axis=1).reshape(idx.shape)
    phys = page_of * PAGE_SIZE + pidx % PAGE_SIZE
    nidx = jnp.clip(idx - past, 0, ds - 1)
    flat_k = pool_k.reshape(-1, A_KV_HEADS, HEAD_DIM)
    flat_v = pool_v.reshape(-1, A_KV_HEADS, HEAD_DIM)
    sel = in_past[..., None, None]
    k_sel = jnp.where(sel, flat_k[phys], gather_rows(k_new, nidx).astype(flat_k.dtype))
    v_sel = jnp.where(sel, flat_v[phys], gather_rows(v_new, nidx).astype(flat_v.dtype))
    return sparse_attend(q, k_sel, v_sel, valid)


def causal_conv(x, buf, w):
    xp = jnp.concatenate([buf.astype(x.dtype), x], axis=1)
    t = x.shape[1]
    y = sum(xp[:, j:j + t] * w[j] for j in range(CONV_W))
    return y, xp[:, -(CONV_W - 1):]


def chunk_gated_delta(q, k, v, g, beta, s0):
    b, t, h, dk = q.shape
    dv = v.shape[-1]
    c = min(DELTA_CHUNK, t)
    n = -(-t // c)
    pad = n * c - t

    def to_chunks(a):
        a = jnp.pad(a, [(0, 0), (0, pad)] + [(0, 0)] * (a.ndim - 2))
        a = a.reshape((b, n, c) + a.shape[2:])
        return jnp.moveaxis(a, 3, 1)

    qc, kc, vc, gc, bc = [to_chunks(a) for a in (q, k, v, g, beta)]
    gcum = jnp.cumsum(gc, axis=-1)
    diff = gcum[..., :, None] - gcum[..., None, :]
    strict = jnp.tril(jnp.ones((c, c), dtype=bool), -1)
    incl = jnp.tril(jnp.ones((c, c), dtype=bool))
    dec_strict = jnp.where(strict, jnp.exp(jnp.where(strict, diff, 0.0)), 0.0)
    dec_incl = jnp.where(incl, jnp.exp(jnp.where(incl, diff, 0.0)), 0.0)
    gamma = jnp.exp(gcum)
    m = bc[..., :, None] * jnp.einsum('bhnid,bhnjd->bhnij', kc, kc) * dec_strict
    rhs = jnp.concatenate([bc[..., None] * vc, (bc * gamma)[..., None] * kc], axis=-1)
    sol = lax.linalg.triangular_solve(m + jnp.eye(c, dtype=m.dtype), rhs, left_side=True,
                                      lower=True, unit_diagonal=True)
    u_v, u_k = sol[..., :dv], sol[..., dv:]
    a_qk = jnp.einsum('bhnid,bhnjd->bhnij', qc, kc) * dec_incl
    q_dec = gamma[..., None] * qc
    k_dec = jnp.exp(gcum[..., -1:] - gcum)[..., None] * kc
    g_last = jnp.exp(gcum[..., -1])

    def step(state, xs):
        u_v_i, u_k_i, a_i, q_i, k_i, gl_i = xs
        u = u_v_i - jnp.einsum('bhck,bhkv->bhcv', u_k_i, state)
        o = jnp.einsum('bhck,bhkv->bhcv', q_i, state) + jnp.einsum('bhij,bhjv->bhiv', a_i, u)
        state = gl_i[..., None, None] * state + jnp.einsum('bhck,bhcv->bhkv', k_i, u)
        return state, o

    xs = tuple(jnp.moveaxis(a, 2, 0) for a in (u_v, u_k, a_qk, q_dec, k_dec, g_last))
    s_final, o = lax.scan(step, s0, xs)
    o = jnp.moveaxis(o, 0, 2).reshape(b, h, n * c, dv)[:, :, :t]
    return jnp.moveaxis(o, 1, 2), s_final


def gated_delta_mixer(qkv, z, beta_logit, a_logit, conv_buf, s0, conv_w, a_log, dt_bias, norm_g):
    b, t, _ = qkv.shape
    conv_out, new_buf = causal_conv(qkv, conv_buf, conv_w)
    conv_out = jax.nn.silu(conv_out).astype(jnp.float32)
    q, k, v = jnp.split(conv_out, 3, axis=-1)
    q = l2norm(q.reshape(b, t, B_HEADS, B_DK)) * (B_DK ** -0.5)
    k = l2norm(k.reshape(b, t, B_HEADS, B_DK))
    v = v.reshape(b, t, B_HEADS, B_DV)
    beta = jax.nn.sigmoid(beta_logit.astype(jnp.float32))
    g = -jnp.exp(a_log.astype(jnp.float32)) * jax.nn.softplus(a_logit.astype(jnp.float32) + dt_bias)
    o, s_new = chunk_gated_delta(q, k, v, g, beta, s0.astype(jnp.float32))
    o = o * lax.rsqrt(jnp.mean(o * o, axis=-1, keepdims=True) + 1e-6) * norm_g
    o = o * jax.nn.silu(z.astype(jnp.float32).reshape(b, t, B_HEADS, B_DV))
    return o.reshape(b, t, B_WIDTH).astype(qkv.dtype), new_buf, s_new.astype(s0.dtype)


def complex_affine_combine(left, right):
    lar, lai, lbr, lbi = left
    rar, rai, rbr, rbi = right
    return (rar * lar - rai * lai, rar * lai + rai * lar,
            rar * lbr - rai * lbi + rbr, rar * lbi + rai * lbr + rbi)


def s5_mixer(u, h0_re, h0_im, a_re, a_im, log_dt, b_re, b_im, c_re, c_im, d_skip, glu_w, glu_b):
    bsz, t, _ = u.shape
    ug = u.astype(jnp.float32).reshape(bsz, t, C_GROUPS, C_GROUP)
    a_re = a_re.astype(jnp.float32)
    a_im = a_im.astype(jnp.float32)
    dt = jnp.exp(log_dt.astype(jnp.float32))[:, None]
    mag = jnp.exp(dt * a_re)
    ab_re = mag * jnp.cos(dt * a_im)
    ab_im = mag * jnp.sin(dt * a_im)
    den = a_re * a_re + a_im * a_im
    nr = ab_re - 1.0
    f_re = (nr * a_re + ab_im * a_im) / den
    f_im = (ab_im * a_re - nr * a_im) / den
    bb_re = f_re[..., None] * b_re - f_im[..., None] * b_im
    bb_im = f_re[..., None] * b_im + f_im[..., None] * b_re
    x_re = jnp.einsum('gpc,btgc->btgp', bb_re, ug)
    x_im = jnp.einsum('gpc,btgc->btgp', bb_im, ug)
    h0r = h0_re.astype(jnp.float32)
    h0i = h0_im.astype(jnp.float32)
    x_re = x_re.at[:, 0].add(ab_re * h0r - ab_im * h0i)
    x_im = x_im.at[:, 0].add(ab_re * h0i + ab_im * h0r)
    lam_re = jnp.broadcast_to(ab_re, x_re.shape)
    lam_im = jnp.broadcast_to(ab_im, x_im.shape)
    _, _, h_re, h_im = lax.associative_scan(complex_affine_combine, (lam_re, lam_im, x_re, x_im), axis=1)
    y = jnp.einsum('gcp,btgp->btgc', c_re, h_re) - jnp.einsum('gcp,btgp->btgc', c_im, h_im)
    y = (y + d_skip.reshape(C_GROUPS, C_GROUP) * ug).reshape(bsz, t, C_WIDTH)
    y = jax.nn.gelu(y)
    out = y * jax.nn.sigmoid(y @ glu_w + glu_b)
    return out.astype(u.dtype), h_re[:, -1].astype(h0_re.dtype), h_im[:, -1].astype(h0_im.dtype)


def moe_ffn(x, router_w, router_bias, w_gate, w_up, w_down):
    n = x.shape[0]
    scores = jax.nn.sigmoid((x @ router_w).astype(jnp.float32))
    biased = scores + router_bias.astype(jnp.float32)
    grouped = biased.reshape(n, N_EXPERT_GROUPS, EXPERTS_PER_GROUP)
    group_score = jnp.sum(lax.top_k(grouped, TOP_K)[0], axis=-1)
    group_sel = jnp.argmax(group_score, axis=-1)
    in_group = (jnp.arange(N_EXPERTS) // EXPERTS_PER_GROUP)[None, :] == group_sel[:, None]
    _, expert_idx = lax.top_k(jnp.where(in_group, biased, -jnp.inf), TOP_K)
    w = jnp.take_along_axis(scores, expert_idx, axis=-1)
    w = w / jnp.sum(w, axis=-1, keepdims=True)
    gates = jnp.sum(jax.nn.one_hot(expert_idx, N_EXPERTS, dtype=jnp.float32) * w[..., None], axis=1)
    h = jax.nn.silu(jnp.einsum('nd,edf->nef', x, w_gate)) * jnp.einsum('nd,edf->nef', x, w_up)
    h = (h * gates[..., None].astype(h.dtype)).reshape(n, N_EXPERTS * D_EXPERT)
    return (h @ w_down.reshape(N_EXPERTS * D_EXPERT, -1)).astype(x.dtype)


def trunk_layer(x, pos, attend, conv_buf, delta_s0, ssm_h0_re, ssm_h0_im, p, router_w, router_bias):
    bsz, t, _ = x.shape
    aq, ak, av, iq, ik, iw, bqkv, bz, bbeta, ba, cu = jnp.split(x @ p['w_in'], in_splits(), axis=-1)
    aq = partial_rope(aq.reshape(bsz, t, A_HEADS, HEAD_DIM), pos)
    ak = partial_rope(ak.reshape(bsz, t, A_KV_HEADS, HEAD_DIM), pos)
    av = av.reshape(bsz, t, A_KV_HEADS, HEAD_DIM)
    iq = partial_rope(iq.reshape(bsz, t, IDX_HEADS, IDX_DIM), pos)
    ik = partial_rope(ik.reshape(bsz, t, 1, IDX_DIM), pos)[:, :, 0]
    iw = iw * ((IDX_HEADS * IDX_DIM) ** -0.5)
    o_attn = attend(aq, ak, av, iq, iw, ik)
    o_delta, conv_new, delta_new = gated_delta_mixer(bqkv, bz, bbeta, ba, conv_buf, delta_s0, p['conv_w'],
                                                     p['delta_a_log'], p['delta_dt_bias'], p['delta_norm_g'])
    o_ssm, ssm_re_new, ssm_im_new = s5_mixer(cu, ssm_h0_re, ssm_h0_im, p['ssm_a_re'], p['ssm_a_im'],
                                             p['ssm_log_dt'], p['ssm_b_re'], p['ssm_b_im'], p['ssm_c_re'],
                                             p['ssm_c_im'], p['ssm_d'], p['glu_w'], p['glu_b'])
    mixed = jnp.concatenate([o_attn, o_delta.astype(o_attn.dtype), o_ssm.astype(o_attn.dtype)], axis=-1) @ p['w_out']
    x = layer_norm(DEEPNORM_ALPHA * x + mixed, p['ln1_g'], p['ln1_b'])
    ffn = moe_ffn(x.reshape(bsz * t, D_MODEL), router_w, router_bias, p['exp_gate'], p['exp_up'],
                  p['exp_down']).reshape(bsz, t, D_MODEL)
    x = layer_norm(DEEPNORM_ALPHA * x + ffn, p['ln2_g'], p['ln2_b'])
    return x, (ak, av, ik, conv_new, delta_new, ssm_re_new, ssm_im_new)


def setup_inputs(seed: int = 0) -> dict:
    key = jax.random.key(seed)
    keys = jax.random.split(key, 48)
    counter = iter(range(48))

    def nk():
        return keys[next(counter)]

    def nrm(shape, scale=1.0):
        return jax.random.normal(nk(), shape, jnp.float32) * scale

    n_pages = PAST_LEN // PAGE_SIZE
    n_phys = (DEC_BATCH * n_pages * 5) // 4
    x_prompt = nrm((BATCH, SEQ, D_MODEL))
    x_sample = nrm((DEC_BATCH, DEC_SEQ, D_MODEL))
    cache_k = nrm((DEPTH, n_phys, PAGE_SIZE, A_KV_HEADS, HEAD_DIM))
    cache_v = nrm((DEPTH, n_phys, PAGE_SIZE, A_KV_HEADS, HEAD_DIM))
    cache_idx_k = nrm((DEPTH, n_phys, PAGE_SIZE, IDX_DIM))
    state_conv = nrm((DEPTH, DEC_BATCH, CONV_W - 1, 3 * B_WIDTH))
    state_delta = nrm((DEPTH, DEC_BATCH, B_HEADS, B_DK, B_DV), 0.05)
    state_ssm_re = nrm((DEPTH, DEC_BATCH, C_GROUPS, C_STATE), 0.5)
    state_ssm_im = nrm((DEPTH, DEC_BATCH, C_GROUPS, C_STATE), 0.5)
    page_table = jax.random.permutation(nk(), n_phys)[: DEC_BATCH * n_pages].reshape(DEC_BATCH, n_pages).astype(jnp.int32)

    w_in = nrm((DEPTH, D_MODEL, IN_WIDTH), D_MODEL ** -0.5)
    w_out = nrm((DEPTH, D_MODEL, D_MODEL), DEEPNORM_BETA * D_MODEL ** -0.5)
    conv_w = nrm((DEPTH, CONV_W, 3 * B_WIDTH), CONV_W ** -0.5)
    delta_a_log = jnp.log(jax.random.uniform(nk(), (DEPTH, B_HEADS), jnp.float32, 1.0, 16.0))
    dt = jnp.exp(jax.random.uniform(nk(), (DEPTH, B_HEADS), jnp.float32, math.log(1e-3), math.log(1e-1)))
    delta_dt_bias = dt + jnp.log(-jnp.expm1(-dt))
    delta_norm_g = 1.0 + nrm((DEPTH, B_DV), 0.02)
    ssm_a_re = -0.5 + nrm((DEPTH, C_GROUPS, C_STATE), 0.01)
    ssm_a_im = jnp.pi * jnp.arange(C_STATE, dtype=jnp.float32) + nrm((DEPTH, C_GROUPS, C_STATE), 0.01)
    ssm_log_dt = jax.random.uniform(nk(), (DEPTH, C_GROUPS), jnp.float32, math.log(1e-3), math.log(1e-1))
    ssm_b_re = nrm((DEPTH, C_GROUPS, C_STATE, C_GROUP), (2 * C_GROUP) ** -0.5)
    ssm_b_im = nrm((DEPTH, C_GROUPS, C_STATE, C_GROUP), (2 * C_GROUP) ** -0.5)
    ssm_c_re = nrm((DEPTH, C_GROUPS, C_GROUP, C_STATE), (2 * C_STATE) ** -0.5)
    ssm_c_im = nrm((DEPTH, C_GROUPS, C_GROUP, C_STATE), (2 * C_STATE) ** -0.5)
    ssm_d = nrm((DEPTH, C_WIDTH))
    glu_w = nrm((DEPTH, C_WIDTH, C_WIDTH), C_WIDTH ** -0.5)
    glu_b = nrm((DEPTH, C_WIDTH), 0.02)
    ln1_g = 1.0 + nrm((DEPTH, D_MODEL), 0.02)
    ln1_b = nrm((DEPTH, D_MODEL), 0.02)
    router_w = nrm((D_MODEL, N_EXPERTS), D_MODEL ** -0.5)
    router_bias = nrm((N_EXPERTS,), 0.01)
    exp_gate = nrm((DEPTH, N_EXPERTS, D_MODEL, D_EXPERT), D_MODEL ** -0.5)
    exp_up = nrm((DEPTH, N_EXPERTS, D_MODEL, D_EXPERT), D_MODEL ** -0.5)
    exp_down = nrm((DEPTH, N_EXPERTS, D_EXPERT, D_MODEL), DEEPNORM_BETA * D_EXPERT ** -0.5)
    ln2_g = 1.0 + nrm((DEPTH, D_MODEL), 0.02)
    ln2_b = nrm((DEPTH, D_MODEL), 0.02)
    return {'x_prompt': x_prompt, 'x_sample': x_sample, 'cache_k': cache_k, 'cache_v': cache_v,
            'cache_idx_k': cache_idx_k, 'state_conv': state_conv, 'state_delta': state_delta,
            'state_ssm_re': state_ssm_re, 'state_ssm_im': state_ssm_im, 'page_table': page_table,
            'w_in': w_in, 'w_out': w_out, 'conv_w': conv_w, 'delta_a_log': delta_a_log,
            'delta_dt_bias': delta_dt_bias, 'delta_norm_g': delta_norm_g, 'ssm_a_re': ssm_a_re,
            'ssm_a_im': ssm_a_im, 'ssm_log_dt': ssm_log_dt, 'ssm_b_re': ssm_b_re, 'ssm_b_im': ssm_b_im,
            'ssm_c_re': ssm_c_re, 'ssm_c_im': ssm_c_im, 'ssm_d': ssm_d, 'glu_w': glu_w, 'glu_b': glu_b,
            'ln1_g': ln1_g, 'ln1_b': ln1_b, 'router_w': router_w, 'router_bias': router_bias,
            'exp_gate': exp_gate, 'exp_up': exp_up, 'exp_down': exp_down, 'ln2_g': ln2_g, 'ln2_b': ln2_b}


def reference(x_prompt, x_sample, cache_k, cache_v, cache_idx_k, state_conv, state_delta, state_ssm_re,
              state_ssm_im, page_table, w_in, w_out, conv_w, delta_a_log, delta_dt_bias, delta_norm_g,
              ssm_a_re, ssm_a_im, ssm_log_dt, ssm_b_re, ssm_b_im, ssm_c_re, ssm_c_im, ssm_d, glu_w, glu_b,
              ln1_g, ln1_b, router_w, router_bias, exp_gate, exp_up, exp_down, ln2_g, ln2_b):
    bp, sp = x_prompt.shape[:2]
    pos_p = jnp.arange(sp, dtype=jnp.int32)
    pos_s = PAST_LEN + jnp.arange(x_sample.shape[1], dtype=jnp.int32)
    zero_conv = jnp.zeros((bp, CONV_W - 1, 3 * B_WIDTH), x_prompt.dtype)
    zero_delta = jnp.zeros((bp, B_HEADS, B_DK, B_DV), x_prompt.dtype)
    zero_ssm = jnp.zeros((bp, C_GROUPS, C_STATE), x_prompt.dtype)
    xp, xs = x_prompt, x_sample
    st_p, st_s = [], []
    for l in range(DEPTH):
        p = {'w_in': w_in[l], 'w_out': w_out[l], 'conv_w': conv_w[l], 'delta_a_log': delta_a_log[l],
             'delta_dt_bias': delta_dt_bias[l], 'delta_norm_g': delta_norm_g[l], 'ssm_a_re': ssm_a_re[l],
             'ssm_a_im': ssm_a_im[l], 'ssm_log_dt': ssm_log_dt[l], 'ssm_b_re': ssm_b_re[l],
             'ssm_b_im': ssm_b_im[l], 'ssm_c_re': ssm_c_re[l], 'ssm_c_im': ssm_c_im[l], 'ssm_d': ssm_d[l],
             'glu_w': glu_w[l], 'glu_b': glu_b[l], 'ln1_g': ln1_g[l], 'ln1_b': ln1_b[l],
             'exp_gate': exp_gate[l], 'exp_up': exp_up[l], 'exp_down': exp_down[l],
             'ln2_g': ln2_g[l], 'ln2_b': ln2_b[l]}
        xp, sp_l = trunk_layer(xp, pos_p, dsa_prompt, zero_conv, zero_delta, zero_ssm, zero_ssm,
                               p, router_w, router_bias)
        attend_s = functools.partial(dsa_sample, pool_k=cache_k[l], pool_v=cache_v[l],
                                     pool_idx=cache_idx_k[l], page_table=page_table)
        xs, ss_l = trunk_layer(xs, pos_s, attend_s, state_conv[l], state_delta[l], state_ssm_re[l],
                               state_ssm_im[l], p, router_w, router_bias)
        st_p.append(sp_l)
        st_s.append(ss_l)
    y_prompt, y_sample = xp, xs
    k_prompt = jnp.stack([s[0] for s in st_p])
    v_prompt = jnp.stack([s[1] for s in st_p])
    idx_k_prompt = jnp.stack([s[2] for s in st_p])
    conv_prompt = jnp.stack([s[3] for s in st_p])
    delta_prompt = jnp.stack([s[4] for s in st_p])
    ssm_re_prompt = jnp.stack([s[5] for s in st_p])
    ssm_im_prompt = jnp.stack([s[6] for s in st_p])
    k_sample = jnp.stack([s[0] for s in st_s])
    v_sample = jnp.stack([s[1] for s in st_s])
    idx_k_sample = jnp.stack([s[2] for s in st_s])
    conv_sample = jnp.stack([s[3] for s in st_s])
    delta_sample = jnp.stack([s[4] for s in st_s])
    ssm_re_sample = jnp.stack([s[5] for s in st_s])
    ssm_im_sample = jnp.stack([s[6] for s in st_s])
    return (y_prompt, y_sample, k_prompt, v_prompt, idx_k_prompt, conv_prompt, delta_prompt, ssm_re_prompt,
            ssm_im_prompt, k_sample, v_sample, idx_k_sample, conv_sample, delta_sample, ssm_re_sample,
            ssm_im_sample)
```

```python
import functools
import math

import jax
import jax.numpy as jnp
import numpy as np
from jax import lax
from jax.experimental import pallas as pl
from jax.experimental.pallas import tpu as pltpu

F32, BF16, I32 = jnp.float32, jnp.bfloat16, jnp.int32

D_MODEL = 4096
BATCH = 4
SEQ = 2048
DEPTH = 2
DEC_BATCH = 8
PAST_LEN = 16384
PAGE_SIZE = 128
N_PAGES = PAST_LEN // PAGE_SIZE
HEAD_DIM = 128
A_WIDTH = D_MODEL // 2
A_HEADS = A_WIDTH // HEAD_DIM
A_KV_HEADS = A_HEADS // 4
KV_WIDTH = A_KV_HEADS * HEAD_DIM
IDX_HEADS = 16
IDX_DIM = 64
TOPK_MAX = 256
ROPE_THETA = 500000.0
ROPE_DIV = 4
B_WIDTH = D_MODEL // 4
B_DK = 128
B_DV = 128
B_HEADS = B_WIDTH // B_DV
CONV_W = 4
C_WIDTH = D_MODEL - A_WIDTH - B_WIDTH
C_GROUP = 16
C_GROUPS = C_WIDTH // C_GROUP
C_STATE = 64
N_EXPERTS = 16
N_EXPERT_GROUPS = 4
EXPERTS_PER_GROUP = N_EXPERTS // N_EXPERT_GROUPS
D_EXPERT = D_MODEL // 4
DEEPNORM_ALPHA = (2 * DEPTH) ** 0.25
LN_EPS = 1e-5

NP = BATCH * SEQ
TM = 512
N_ALL = NP + TM
LANES = 128
VMEM_LIMIT = 56 * 1024 * 1024
NEG = -1e30
INT_MIN = -(2 ** 31)
S5_CHUNK = 16
S5_ROWS = NP // S5_CHUNK
DELTA_CHUNK = 128

SM_IK, SM_IW, SM_BETA, SM_A = 0, 64, 80, 88


def _cp(sem, vmem=VMEM_LIMIT):
    return pltpu.CompilerParams(dimension_semantics=sem, vmem_limit_bytes=vmem)


def _nt(a, b, precision=None):
    return lax.dot_general(a, b, (((1,), (1,)), ((), ())), preferred_element_type=F32, precision=precision)


def _mm(a, b):
    return jnp.dot(a.astype(BF16), b.astype(BF16), preferred_element_type=F32)


def _mm_hi(a, b):
    return jnp.dot(a, b, preferred_element_type=F32, precision=lax.Precision.HIGHEST)


def _sigmoid(x):
    return 1.0 / (1.0 + jnp.exp(-x))


def _silu(x):
    return x * _sigmoid(x)


def _softplus(x):
    return jnp.maximum(x, 0.0) + jnp.log(1.0 + jnp.exp(-jnp.abs(x)))


def _gelu_tanh(x):
    return 0.5 * x * (1.0 + jnp.tanh(math.sqrt(2.0 / math.pi) * (x + 0.044715 * (x * x * x))))


def _sortable(x):
    b = pltpu.bitcast(x + 0.0, I32)
    return b ^ ((b >> 31) & 0x7FFFFFFF)


def _proj_kernel(x_ref, w_ref, *rest, shift, precision):
    o_ref = rest[-1]
    acc = jnp.dot(x_ref[...], w_ref[...], preferred_element_type=F32, precision=precision)
    if shift:
        c_ref, a_ref, b_ref = rest[:3]
        tn = acc.shape[1]
        rep = tn // LANES
        c = jnp.tile(c_ref[...], (1, rep))
        a = jnp.tile(a_ref[...], (1, rep))
        b = jnp.tile(b_ref[...], (1, rep))
        acc = acc * c + pltpu.roll(acc, tn - shift, 1) * a + pltpu.roll(acc, shift, 1) * b
    o_ref[...] = acc


def _proj(x, w, tables=None, shift=0, tn=512, tm=TM, precision=None):
    n, k = x.shape
    ncol = w.shape[1]
    tn = min(tn, ncol)
    grid = (ncol // tn, n // tm)
    in_specs = [pl.BlockSpec((tm, k), lambda j, i: (i, 0)),
                pl.BlockSpec((k, tn), lambda j, i: (0, j))]
    args = [x, w]
    if shift:
        for t in tables:
            in_specs.append(pl.BlockSpec((tm, LANES), lambda j, i: (i, 0)))
            args.append(t)
    return pl.pallas_call(
        functools.partial(_proj_kernel, shift=shift, precision=precision),
        grid=grid, in_specs=in_specs,
        out_specs=pl.BlockSpec((tm, tn), lambda j, i: (i, j)),
        out_shape=jax.ShapeDtypeStruct((n, ncol), F32),
        compiler_params=_cp(("parallel", "arbitrary")),
        name=f"proj_s{shift}_n{ncol}_m{n}",
    )(*args)


def _rope_tables():
    pos = jnp.concatenate([jnp.tile(jnp.arange(SEQ, dtype=I32), BATCH),
                           jnp.full((DEC_BATCH,), PAST_LEN, I32),
                           jnp.zeros((N_ALL - NP - DEC_BATCH,), I32)])

    def table(head_dim):
        rot = head_dim // ROPE_DIV
        half = rot // 2
        inv_freq = ROPE_THETA ** (-jnp.arange(half, dtype=F32) * 2.0 / rot)
        ang = pos.astype(F32)[:, None] * inv_freq[None, :]
        cos, sin = jnp.cos(ang), jnp.sin(ang)
        one = jnp.ones((N_ALL, head_dim - rot), F32)
        zero = jnp.zeros((N_ALL, head_dim - rot), F32)
        zh = jnp.zeros((N_ALL, half), F32)
        rep = LANES // head_dim
        c = jnp.tile(jnp.concatenate([cos, cos, one], axis=1), (1, rep))
        a = jnp.tile(jnp.concatenate([-sin, zh, zero], axis=1), (1, rep))
        b = jnp.tile(jnp.concatenate([zh, sin, zero], axis=1), (1, rep))
        return (c, a, b), half

    t128, s128 = table(HEAD_DIM)
    t64, s64 = table(IDX_DIM)
    lane = jnp.arange(LANES)[None, :]
    is_key = lane < IDX_DIM
    scale = jnp.where((lane >= SM_IW) & (lane < SM_IW + IDX_HEADS), (IDX_HEADS * IDX_DIM) ** -0.5, 1.0).astype(F32)
    tsm = (jnp.where(is_key, t64[0], scale), jnp.where(is_key, t64[1], 0.0), jnp.where(is_key, t64[2], 0.0))
    return (t128, s128), (t64, s64), (tsm, s64)


def _kth_largest_key(count_ge, k, shape):
    cnt = count_ge(jnp.zeros(shape, I32))
    prefix = jnp.where(cnt >= k, jnp.zeros(shape, I32), jnp.full(shape, INT_MIN, I32))

    def body(i, prefix):
        cand = prefix | lax.shift_left(jnp.int32(1), (30 - i).astype(I32))
        return jnp.where(count_ge(cand) >= k, cand, prefix)

    return lax.fori_loop(0, 31, body, prefix)


def _dsa_prompt_kernel(aq_ref, iq_ref, smq_ref, ak_ref, av_ref, smk_ref, o_ref, key_ref, bias_ref, *, qblk, seq):
    qb = pl.program_id(1)
    ncol = seq // LANES
    ik = smk_ref[:, SM_IK:SM_IK + IDX_DIM].astype(BF16)
    w = smq_ref[:, SM_IW:SM_IW + IDX_HEADS]
    for h in range(IDX_HEADS):
        qh = iq_ref[:, h * IDX_DIM:(h + 1) * IDX_DIM].astype(BF16)
        term = w[:, h:h + 1] * jnp.maximum(_nt(qh, ik), 0.0)
        if h == 0:
            bias_ref[...] = term
        else:
            bias_ref[...] += term
    row = qb * qblk + lax.broadcasted_iota(I32, (qblk, seq), 0)
    col = lax.broadcasted_iota(I32, (qblk, seq), 1)
    causal = col <= row
    key_ref[...] = jnp.where(causal, _sortable(bias_ref[...]), INT_MIN)

    def count_ge(cand):
        tot = jnp.zeros((qblk, LANES), I32)
        for c in range(ncol):
            tot = tot + (key_ref[:, c * LANES:(c + 1) * LANES] >= cand).astype(I32)
        return jnp.sum(tot, axis=1, keepdims=True)

    thr = _kth_largest_key(count_ge, TOPK_MAX, (qblk, 1))
    sel = causal & (key_ref[...] >= thr)
    bias_ref[...] = jnp.where(sel, 0.0, NEG)

    scale = HEAD_DIM ** -0.5
    for g in range(A_KV_HEADS):
        kg = ak_ref[:, g * HEAD_DIM:(g + 1) * HEAD_DIM].astype(BF16)
        vg = av_ref[:, g * HEAD_DIM:(g + 1) * HEAD_DIM].astype(BF16)
        for r in range(A_HEADS // A_KV_HEADS):
            h = g * (A_HEADS // A_KV_HEADS) + r
            q = aq_ref[:, h * HEAD_DIM:(h + 1) * HEAD_DIM].astype(BF16)
            s = _nt(q, kg) * scale + bias_ref[...]
            m = jnp.max(s, axis=1, keepdims=True)
            p = jnp.exp(s - m)
            l = jnp.sum(p, axis=1, keepdims=True)
            o = jnp.dot(p.astype(BF16), vg, preferred_element_type=F32) / l
            o_ref[:, h * HEAD_DIM:(h + 1) * HEAD_DIM] = o.astype(o_ref.dtype)


def _dsa_prompt(qa, iq, sm, pl_main, *, batch=BATCH, seq=SEQ, qblk=128):
    nq = seq // qblk
    return pl.pallas_call(
        functools.partial(_dsa_prompt_kernel, qblk=qblk, seq=seq),
        grid=(batch, nq),
        in_specs=[
            pl.BlockSpec((qblk, A_WIDTH), lambda b, q: (b * nq + q, 0)),
            pl.BlockSpec((qblk, IDX_HEADS * IDX_DIM), lambda b, q: (b * nq + q, 0)),
            pl.BlockSpec((qblk, LANES), lambda b, q: (b * nq + q, 0)),
            pl.BlockSpec((seq, KV_WIDTH), lambda b, q: (b, A_WIDTH // KV_WIDTH)),
            pl.BlockSpec((seq, KV_WIDTH), lambda b, q: (b, 0)),
            pl.BlockSpec((seq, LANES), lambda b, q: (b, 0)),
        ],
        out_specs=pl.BlockSpec((qblk, A_WIDTH), lambda b, q: (b * nq + q, 0)),
        out_shape=jax.ShapeDtypeStruct((batch * seq, A_WIDTH), BF16),
        scratch_shapes=[pltpu.VMEM((qblk, seq), I32), pltpu.VMEM((qblk, seq), F32)],
        compiler_params=_cp(("parallel", "arbitrary")),
        name="dsa_prompt",
    )(qa, iq, sm, qa, pl_main, sm)


def _dsa_sample_scores_kernel(pt_ref, iq_ref, w_ref, page_ref, o_ref):
    p = pl.program_id(1)
    d = _nt(iq_ref[...], page_ref[...], precision=lax.Precision.HIGHEST)
    o_ref[pl.ds(p, 1), :] = jnp.sum(w_ref[...] * jnp.maximum(d, 0.0), axis=0, keepdims=True)


def _dsa_sample_scores(page_table, iq_s, w_s, cache_idx, layer):
    return pl.pallas_call(
        _dsa_sample_scores_kernel,
        grid_spec=pltpu.PrefetchScalarGridSpec(
            num_scalar_prefetch=1, grid=(DEC_BATCH, N_PAGES),
            in_specs=[
                pl.BlockSpec((None, IDX_HEADS, IDX_DIM), lambda b, p, pt: (b, 0, 0)),
                pl.BlockSpec((None, IDX_HEADS, 1), lambda b, p, pt: (b, 0, 0)),
                pl.BlockSpec((None, None, PAGE_SIZE, IDX_DIM), lambda b, p, pt: (layer, pt[b, p], 0, 0)),
            ],
            out_specs=pl.BlockSpec((None, N_PAGES, PAGE_SIZE), lambda b, p, pt: (b, 0, 0)),
        ),
        out_shape=jax.ShapeDtypeStruct((DEC_BATCH, N_PAGES, PAGE_SIZE), F32),
        compiler_params=_cp(("parallel", "arbitrary")),
        name="dsa_sample_scores",
    )(page_table, iq_s, w_s, cache_idx)


def _dsa_sample_attend_kernel(pt_ref, sc_ref, iq_ref, w_ref, ikn_ref, q_ref, kn_ref, vn_ref, kp_ref, vp_ref,
                              o_ref, bias_ref, m_ref, l_ref, acc_ref, nb_ref):
    p = pl.program_id(1)
    hi = lax.Precision.HIGHEST
    rep = A_HEADS // A_KV_HEADS
    grp = lax.broadcasted_iota(I32, (A_HEADS, 1), 0) // rep
    scale = HEAD_DIM ** -0.5

    @pl.when(p == 0)
    def _():
        key = _sortable(sc_ref[...])
        d_new = jnp.sum(iq_ref[...] * ikn_ref[...], axis=1, keepdims=True)
        s_new = jnp.sum(w_ref[...] * jnp.maximum(d_new, 0.0), axis=0, keepdims=True)
        key_new = _sortable(s_new)

        def count_ge(cand):
            c = jnp.sum((key >= cand).astype(I32), axis=0, keepdims=True)
            return jnp.sum(c, axis=1, keepdims=True) + (key_new >= cand).astype(I32)

        thr = _kth_largest_key(count_ge, TOPK_MAX, (1, 1))
        bias_ref[...] = jnp.where(key >= thr, 0.0, NEG)
        nb_ref[...] = jnp.where(key_new >= thr, 0.0, NEG)
        m_ref[...] = jnp.full(m_ref.shape, NEG, F32)
        l_ref[...] = jnp.zeros(l_ref.shape, F32)
        acc_ref[...] = jnp.zeros(acc_ref.shape, F32)

    q = q_ref[...]
    s = jnp.zeros((A_HEADS, PAGE_SIZE), F32)
    for g in range(A_KV_HEADS):
        sg = _nt(q, kp_ref[:, g * HEAD_DIM:(g + 1) * HEAD_DIM], precision=hi)
        s = s + jnp.where(grp == g, sg, 0.0)
    s = s * scale + bias_ref[pl.ds(p, 1), :]
    m_new = jnp.maximum(m_ref[...], jnp.max(s, axis=1, keepdims=True))
    alpha = jnp.exp(m_ref[...] - m_new)
    pr = jnp.exp(s - m_new)
    pv = jnp.zeros((A_HEADS, HEAD_DIM), F32)
    for g in range(A_KV_HEADS):
        pvg = jnp.dot(pr, vp_ref[:, g * HEAD_DIM:(g + 1) * HEAD_DIM], preferred_element_type=F32, precision=hi)
        pv = pv + jnp.where(grp == g, pvg, 0.0)
    l_ref[...] = alpha * l_ref[...] + jnp.sum(pr, axis=1, keepdims=True)
    acc_ref[...] = alpha * acc_ref[...] + pv
    m_ref[...] = m_new

    @pl.when(p == pl.num_programs(1) - 1)
    def _():
        kn = kn_ref[...]
        vn = vn_ref[...]
        sn = jnp.zeros((A_HEADS, 1), F32)
        vsel = jnp.zeros((A_HEADS, HEAD_DIM), F32)
        for g in range(A_KV_HEADS):
            sn = sn + jnp.where(grp == g, jnp.sum(q * kn[g:g + 1, :], axis=1, keepdims=True), 0.0)
            vsel = vsel + jnp.where(grp == g, vn[g:g + 1, :], 0.0)
        sn = sn * scale + nb_ref[...]
        m_fin = jnp.maximum(m_ref[...], sn)
        alpha = jnp.exp(m_ref[...] - m_fin)
        pn = jnp.exp(sn - m_fin)
        l = alpha * l_ref[...] + pn
        o_ref[...] = (alpha * acc_ref[...] + pn * vsel) / l


def _dsa_sample_attend(page_table, scores, iq_s, w_s, ikn, q_s, kn, vn, cache_k, cache_v, layer):
    bm = lambda shape: pl.BlockSpec((None,) + shape, lambda b, p, pt: (b,) + (0,) * len(shape))
    page = pl.BlockSpec((None, None, PAGE_SIZE, KV_WIDTH), lambda b, p, pt: (layer, pt[b, p], 0, 0))
    return pl.pallas_call(
        _dsa_sample_attend_kernel,
        grid_spec=pltpu.PrefetchScalarGridSpec(
            num_scalar_prefetch=1, grid=(DEC_BATCH, N_PAGES),
            in_specs=[bm((N_PAGES, PAGE_SIZE)), bm((IDX_HEADS, IDX_DIM)), bm((IDX_HEADS, 1)), bm((1, IDX_DIM)),
                      bm((A_HEADS, HEAD_DIM)), bm((A_KV_HEADS, HEAD_DIM)), bm((A_KV_HEADS, HEAD_DIM)), page, page],
            out_specs=bm((A_HEADS, HEAD_DIM)),
            scratch_shapes=[pltpu.VMEM((N_PAGES, PAGE_SIZE), F32), pltpu.VMEM((A_HEADS, 1), F32),
                            pltpu.VMEM((A_HEADS, 1), F32), pltpu.VMEM((A_HEADS, HEAD_DIM), F32),
                            pltpu.VMEM((1, 1), F32)],
        ),
        out_shape=jax.ShapeDtypeStruct((DEC_BATCH, A_HEADS, HEAD_DIM), F32),
        compiler_params=_cp(("parallel", "arbitrary")),
        name="dsa_sample_attend",
    )(page_table, scores, iq_s, w_s, ikn, q_s, kn, vn, cache_k, cache_v)


def _l2norm(x):
    return x * lax.rsqrt(jnp.sum(x * x, axis=-1, keepdims=True) + 1e-6)


def _unit_lower_inverse(m):
    c = m.shape[0]
    ri = lax.broadcasted_iota(I32, (c, c), 0)
    ci = lax.broadcasted_iota(I32, (c, c), 1)
    eye = (ri == ci).astype(F32)
    d = jnp.where((ri // 16) == (ci // 16), m, 0.0)
    m2 = _mm_hi(d, d)
    m4 = _mm_hi(m2, m2)
    m8 = _mm_hi(m4, m4)
    t = _mm_hi(_mm_hi(_mm_hi(eye - d, eye + m2), eye + m4), eye + m8)
    n = 16
    while n < c:
        lower_left = ((ri // n) == (ci // n) + 1) & (((ri // n) % 2) == 1)
        t = t - _mm_hi(_mm_hi(t, jnp.where(lower_left, m, 0.0)), t)
        n *= 2
    return t


def _delta_prompt_kernel(alog_ref, dtb_ref, q_ref, k_ref, v_ref, z_ref, sm_ref, wq_ref, wk_ref, wv_ref, ng_ref,
                         o_ref, s_ref, *, seq):
    h = pl.program_id(1)
    c = DELTA_CHUNK
    rowc = lax.broadcasted_iota(I32, (c, LANES), 0)
    ri = lax.broadcasted_iota(I32, (c, c), 0)
    ci = lax.broadcasted_iota(I32, (c, c), 1)
    incl = ci <= ri
    strict = ci < ri
    lane = lax.broadcasted_iota(I32, (c, LANES), 1)
    neg_a = -jnp.exp(jnp.full((1, 1), alog_ref[h], F32))
    dtb = dtb_ref[h]
    s_ref[...] = jnp.zeros(s_ref.shape, F32)

    def chunk(ic, carry):
        r0 = pl.multiple_of(ic * c, c)
        rp = pl.multiple_of(jnp.maximum(ic - 1, 0) * c, c)
        has_prev = (ic > 0).astype(F32)

        def conv_silu(x_ref, w_ref):
            xc = x_ref[pl.ds(r0, c), :]
            xp = x_ref[pl.ds(rp, c), :] * has_prev
            y = xc * w_ref[CONV_W - 1:CONV_W, :]
            for j in range(1, CONV_W):
                sh = jnp.where(rowc >= j, pltpu.roll(xc, j, 0), pltpu.roll(xp, j, 0))
                y = y + sh * w_ref[CONV_W - 1 - j:CONV_W - j, :]
            return _silu(y)

        q = _l2norm(conv_silu(q_ref, wq_ref)) * (B_DK ** -0.5)
        k = _l2norm(conv_silu(k_ref, wk_ref))
        v = conv_silu(v_ref, wv_ref)
        sm = sm_ref[pl.ds(r0, c), :]
        beta = _sigmoid(jnp.sum(jnp.where(lane == SM_BETA + h, sm, 0.0), axis=1, keepdims=True))
        a_logit = jnp.sum(jnp.where(lane == SM_A + h, sm, 0.0), axis=1, keepdims=True)
        g = neg_a * _softplus(a_logit + dtb)
        gc = jnp.broadcast_to(g, (c, LANES))
        s = 1
        while s < c:
            gc = gc + jnp.where(rowc >= s, pltpu.roll(gc, s, 0), 0.0)
            s *= 2
        diff = gc - gc.T
        dec = jnp.where(incl, jnp.exp(jnp.where(incl, diff, 0.0)), 0.0)
        kb = k.astype(BF16)
        m = beta * _nt(kb, kb) * jnp.where(strict, dec, 0.0)
        a_qk = _nt(q.astype(BF16), kb) * dec
        t = _unit_lower_inverse(m)
        gamma = jnp.exp(gc)
        g_last = gc[c - 1:c, :]
        u_v = _mm(t, beta * v)
        u_k = _mm(t, (beta * gamma) * k)
        state = s_ref[...]
        u = u_v - _mm(u_k, state)
        o = _mm(gamma * q, state) + _mm(a_qk, u)
        k_dec = jnp.exp(g_last - gc) * k
        s_ref[...] = jnp.exp(g_last) * state + _mm(k_dec.T, u)
        o = o * lax.rsqrt(jnp.mean(o * o, axis=-1, keepdims=True) + 1e-6) * ng_ref[...]
        o = o * _silu(z_ref[pl.ds(r0, c), :])
        o_ref[pl.ds(r0, c), :] = o.astype(o_ref.dtype)
        return carry

    lax.fori_loop(0, seq // c, chunk, 0)


def _delta_prompt(pl_main, sm, conv_w, a_log, dt_bias, norm_g, *, batch=BATCH, seq=SEQ):
    col = lambda base: pl.BlockSpec((seq, B_DK), lambda b, h: (b, base + h))
    wcol = lambda base: pl.BlockSpec((CONV_W, B_DK), lambda b, h: (0, base + h))
    smem = pl.BlockSpec(memory_space=pltpu.SMEM)
    return pl.pallas_call(
        functools.partial(_delta_prompt_kernel, seq=seq),
        grid=(batch, B_HEADS),
        in_specs=[smem, smem, col(4), col(12), col(20), col(28),
                  pl.BlockSpec((seq, LANES), lambda b, h: (b, 0)),
                  wcol(0), wcol(8), wcol(16),
                  pl.BlockSpec((1, B_DV), lambda b, h: (0, 0))],
        out_specs=[pl.BlockSpec((seq, B_DV), lambda b, h: (b, h)),
                   pl.BlockSpec((None, None, B_DK, B_DV), lambda b, h: (b, h, 0, 0))],
        out_shape=[jax.ShapeDtypeStruct((batch * seq, B_WIDTH), BF16),
                   jax.ShapeDtypeStruct((batch, B_HEADS, B_DK, B_DV), F32)],
        compiler_params=_cp(("parallel", "arbitrary")),
        name="delta_prompt",
    )(a_log, dt_bias, pl_main, pl_main, pl_main, pl_main, sm, conv_w, conv_w, conv_w, norm_g.reshape(1, B_DV))


def _delta_sample_kernel(alog_ref, dtb_ref, x_ref, z_ref, sm_ref, cb_ref, w_ref, ng_ref, s0_ref, o_ref, s_ref):
    y = x_ref[...] * w_ref[CONV_W - 1:CONV_W, :]
    for j in range(CONV_W - 1):
        y = y + cb_ref[j:j + 1, :] * w_ref[j:j + 1, :]
    y = _silu(y)
    sm = sm_ref[...]
    ri = lax.broadcasted_iota(I32, (B_DK, B_DK), 0)
    ci = lax.broadcasted_iota(I32, (B_DK, B_DK), 1)
    eye = (ri == ci).astype(F32)
    col = lambda r: jnp.sum(eye * r, axis=1, keepdims=True)
    for h in range(B_HEADS):
        q = _l2norm(y[:, h * B_DK:(h + 1) * B_DK]) * (B_DK ** -0.5)
        k = _l2norm(y[:, B_WIDTH + h * B_DK:B_WIDTH + (h + 1) * B_DK])
        v = y[:, 2 * B_WIDTH + h * B_DV:2 * B_WIDTH + (h + 1) * B_DV]
        beta = _sigmoid(sm[:, SM_BETA + h:SM_BETA + h + 1])
        neg_a = -jnp.exp(jnp.full((1, 1), alog_ref[h], F32))
        a = jnp.exp(neg_a * _softplus(sm[:, SM_A + h:SM_A + h + 1] + dtb_ref[h]))
        state = s0_ref[h]
        k_col, q_col = col(k), col(q)
        u = beta * (v - a * jnp.sum(state * k_col, axis=0, keepdims=True))
        new = a * state + k_col * u
        s_ref[h] = new
        o = jnp.sum(new * q_col, axis=0, keepdims=True)
        o = o * lax.rsqrt(jnp.mean(o * o, axis=-1, keepdims=True) + 1e-6) * ng_ref[...]
        o_ref[:, h * B_DV:(h + 1) * B_DV] = o * _silu(z_ref[:, h * B_DV:(h + 1) * B_DV])


def _delta_sample(x_s, z_s, sm_s, conv_buf, conv_w, a_log, dt_bias, norm_g, s0):
    nb = x_s.shape[0]
    smem = pl.BlockSpec(memory_space=pltpu.SMEM)
    per_b = lambda *shape: pl.BlockSpec((None,) + shape, lambda b: (b,) + (0,) * len(shape))
    full = lambda *shape: pl.BlockSpec(shape, lambda b: (0,) * len(shape))
    return pl.pallas_call(
        _delta_sample_kernel,
        grid=(nb,),
        in_specs=[smem, smem, per_b(1, 3 * B_WIDTH), per_b(1, B_WIDTH), per_b(1, LANES),
                  per_b(CONV_W - 1, 3 * B_WIDTH), full(CONV_W, 3 * B_WIDTH), full(1, B_DV),
                  per_b(B_HEADS, B_DK, B_DV)],
        out_specs=[per_b(1, B_WIDTH), per_b(B_HEADS, B_DK, B_DV)],
        out_shape=[jax.ShapeDtypeStruct((nb, 1, B_WIDTH), F32),
                   jax.ShapeDtypeStruct((nb, B_HEADS, B_DK, B_DV), F32)],
        compiler_params=_cp(("parallel",)),
        name="delta_sample",
    )(a_log, dt_bias, x_s, z_s, sm_s, conv_buf, conv_w, norm_g.reshape(1, B_DV), s0)


def _s5_discretise(a_re, a_im, log_dt, b_re, b_im):
    dt = jnp.exp(log_dt)[:, None]
    mag = jnp.exp(dt * a_re)
    ab_re = mag * jnp.cos(dt * a_im)
    ab_im = mag * jnp.sin(dt * a_im)
    den = a_re * a_re + a_im * a_im
    nr = ab_re - 1.0
    f_re = (nr * a_re + ab_im * a_im) / den
    f_im = (ab_im * a_re - nr * a_im) / den
    bb_re = f_re[..., None] * b_re - f_im[..., None] * b_im
    bb_im = f_re[..., None] * b_im + f_im[..., None] * b_re
    return dt, ab_re, ab_im, bb_re, bb_im


def _s5_prompt_operands(a_re, a_im, log_dt, b_re, b_im, c_re, c_im, d_skip):
    hi = lax.Precision.HIGHEST
    L = S5_CHUNK
    dt, _, _, bb_re, bb_im = _s5_discretise(a_re, a_im, log_dt, b_re, b_im)

    def power(n):
        n = jnp.asarray(n, F32)[:, None, None]
        mag = jnp.exp(n * (dt * a_re)[None])
        ang = n * (dt * a_im)[None]
        return mag * jnp.cos(ang), mag * jnp.sin(ang)

    pw_re, pw_im = power(np.arange(L + 1))
    e_re = pw_re[..., None] * bb_re[None] - pw_im[..., None] * bb_im[None]
    e_im = pw_re[..., None] * bb_im[None] + pw_im[..., None] * bb_re[None]
    kern = (jnp.einsum('gop,tgpc->gtoc', c_re, e_re[:L], precision=hi)
            - jnp.einsum('gop,tgpc->gtoc', c_im, e_im[:L], precision=hi))
    jj, ii = np.meshgrid(np.arange(L), np.arange(L), indexing='ij')
    toep = jnp.where((ii >= jj)[None, :, :, None, None], kern[:, np.clip(ii - jj, 0, L - 1)], 0.0)
    dmat = jnp.eye(C_GROUP, dtype=F32)[None] * d_skip.reshape(C_GROUPS, 1, C_GROUP)
    toep = toep + jnp.where((ii == jj)[None, :, :, None, None], dmat[:, None, None], 0.0)
    toep = jnp.transpose(toep, (0, 1, 4, 2, 3)).reshape(C_GROUPS, L * C_GROUP, L * C_GROUP)
    rev = np.arange(L - 1, -1, -1)
    wend = jnp.concatenate([jnp.transpose(e_re[rev], (1, 0, 3, 2)), jnp.transpose(e_im[rev], (1, 0, 3, 2))],
                           axis=-1).reshape(C_GROUPS, L * C_GROUP, 2 * C_STATE)
    cp_re = c_re[None] * pw_re[1:, :, None, :] - c_im[None] * pw_im[1:, :, None, :]
    cp_im = c_re[None] * pw_im[1:, :, None, :] + c_im[None] * pw_re[1:, :, None, :]
    wc = jnp.concatenate([jnp.transpose(cp_re, (1, 3, 0, 2)), -jnp.transpose(cp_im, (1, 3, 0, 2))],
                         axis=1).reshape(C_GROUPS, 2 * C_STATE, L * C_GROUP)
    steps = L * (2 ** np.arange(int(math.log2(SEQ // L))))
    lr, li = power(steps)
    lp = jnp.stack([jnp.concatenate([lr, lr], -1), jnp.concatenate([-li, li], -1)], axis=2)
    lp = jnp.transpose(lp, (1, 0, 2, 3))
    return toep.astype(BF16), wend.astype(BF16), wc.astype(BF16), lp


def _s5_prompt_kernel(u_ref, toep_ref, wend_ref, wc_ref, lp_ref, y_ref, hf_ref, *, batch, nchunk):
    u = u_ref[...]
    rows = batch * nchunk
    h = jnp.dot(u, wend_ref[...], preferred_element_type=F32)
    m = lax.broadcasted_iota(I32, (rows, 2 * C_STATE), 0) % nchunk
    s, si = 1, 0
    while s < nchunk:
        hs = jnp.where(m >= s, pltpu.roll(h, s, 0), 0.0)
        h = h + lp_ref[si, 0:1, :] * hs + lp_ref[si, 1:2, :] * pltpu.roll(hs, C_STATE, 1)
        s *= 2
        si += 1
    h_start = jnp.where(m >= 1, pltpu.roll(h, 1, 0), 0.0)
    y = jnp.dot(u, toep_ref[...], preferred_element_type=F32)
    y = y + jnp.dot(h_start.astype(BF16), wc_ref[...], preferred_element_type=F32)
    y_ref[...] = _gelu_tanh(y)
    hf_ref[...] = jnp.zeros(hf_ref.shape, F32)
    for b in range(batch):
        hf_ref[b:b + 1, :] = h[(b + 1) * nchunk - 1:(b + 1) * nchunk, :]


def _s5_prompt(u_g, toep, wend, wc, lp, *, batch=BATCH, seq=SEQ):
    L = S5_CHUNK
    nchunk = seq // L
    rows = batch * nchunk
    per_g = lambda *shape: pl.BlockSpec((None,) + shape, lambda g: (g,) + (0,) * len(shape))
    return pl.pallas_call(
        functools.partial(_s5_prompt_kernel, batch=batch, nchunk=nchunk),
        grid=(C_GROUPS,),
        in_specs=[per_g(rows, L * C_GROUP), per_g(L * C_GROUP, L * C_GROUP), per_g(L * C_GROUP, 2 * C_STATE),
                  per_g(2 * C_STATE, L * C_GROUP), per_g(lp.shape[1], 2, 2 * C_STATE)],
        out_specs=[per_g(rows, L * C_GROUP), per_g(8, 2 * C_STATE)],
        out_shape=[jax.ShapeDtypeStruct((C_GROUPS, rows, L * C_GROUP), F32),
                   jax.ShapeDtypeStruct((C_GROUPS, 8, 2 * C_STATE), F32)],
        compiler_params=_cp(("parallel",)),
        name="s5_prompt",
    )(u_g, toep, wend, wc, lp)


S5_BLK = LANES // C_GROUP


def _s5_sample_operands(a_re, a_im, log_dt, b_re, b_im, c_re, c_im, d_skip):
    _, ab_re, ab_im, bb_re, bb_im = _s5_discretise(a_re, a_im, log_dt, b_re, b_im)
    nblk = C_GROUPS // S5_BLK
    eye = jnp.eye(S5_BLK, dtype=F32)

    def in_map(bb):
        bb = bb.reshape(nblk, S5_BLK, C_STATE, C_GROUP)
        return jnp.einsum('ngpc,hg->nhcgp', bb, eye).reshape(nblk, LANES, S5_BLK * C_STATE)

    def out_map(cc):
        cc = cc.reshape(nblk, S5_BLK, C_GROUP, C_STATE)
        return jnp.einsum('ngop,hg->ngpho', cc, eye).reshape(nblk, S5_BLK * C_STATE, LANES)

    flat = lambda x: x.reshape(nblk, 1, S5_BLK * C_STATE)
    return (in_map(bb_re), in_map(bb_im), out_map(c_re), out_map(c_im), flat(ab_re), flat(ab_im),
            d_skip.reshape(nblk, 1, LANES))


def _s5_sample_kernel(u_ref, hr_ref, hi_ref, bre_ref, bim_ref, cre_ref, cim_ref, lr_ref, li_ref, d_ref,
                      y_ref, nr_ref, ni_ref):
    u = u_ref[...]
    x_re = _mm_hi(u, bre_ref[...])
    x_im = _mm_hi(u, bim_ref[...])
    h_re = lr_ref[...] * hr_ref[...] - li_ref[...] * hi_ref[...] + x_re
    h_im = lr_ref[...] * hi_ref[...] + li_ref[...] * hr_ref[...] + x_im
    nr_ref[...] = h_re
    ni_ref[...] = h_im
    y = _mm_hi(h_re, cre_ref[...]) - _mm_hi(h_im, cim_ref[...]) + d_ref[...] * u
    y_ref[...] = _gelu_tanh(y)


def _s5_sample(u_s, h_re, h_im, ops):
    nb = u_s.shape[0]
    nblk = C_GROUPS // S5_BLK
    w = S5_BLK * C_STATE
    per_n = lambda *shape: pl.BlockSpec((None,) + shape, lambda n: (n,) + (0,) * len(shape))
    act = lambda width: pl.BlockSpec((nb, width), lambda n: (0, n))
    return pl.pallas_call(
        _s5_sample_kernel,
        grid=(nblk,),
        in_specs=[act(LANES), act(w), act(w), per_n(LANES, w), per_n(LANES, w), per_n(w, LANES), per_n(w, LANES),
                  per_n(1, w), per_n(1, w), per_n(1, LANES)],
        out_specs=[act(LANES), act(w), act(w)],
        out_shape=[jax.ShapeDtypeStruct((nb, C_WIDTH), F32),
                   jax.ShapeDtypeStruct((nb, C_GROUPS * C_STATE), F32),
                   jax.ShapeDtypeStruct((nb, C_GROUPS * C_STATE), F32)],
        compiler_params=_cp(("parallel",)),
        name="s5_sample",
    )(u_s, h_re, h_im, *ops)


def _glu_kernel(y_ref, w_ref, b_ref, o_ref, *, precision):
    y = y_ref[...]
    gate = jnp.dot(y.astype(w_ref.dtype), w_ref[...], preferred_element_type=F32, precision=precision) + b_ref[...]
    o_ref[...] = (y * _sigmoid(gate)).astype(o_ref.dtype)


def _glu(y, w, b, tm=TM, precision=None, out_dtype=BF16):
    n = y.shape[0]
    return pl.pallas_call(
        functools.partial(_glu_kernel, precision=precision),
        grid=(n // tm,),
        in_specs=[pl.BlockSpec((tm, C_WIDTH), lambda i: (i, 0)),
                  pl.BlockSpec((C_WIDTH, C_WIDTH), lambda i: (0, 0)),
                  pl.BlockSpec((1, C_WIDTH), lambda i: (0, 0))],
        out_specs=pl.BlockSpec((tm, C_WIDTH), lambda i: (i, 0)),
        out_shape=jax.ShapeDtypeStruct((n, C_WIDTH), out_dtype),
        compiler_params=_cp(("parallel",)),
        name=f"s5_glu_m{n}",
    )(y, w, b.reshape(1, C_WIDTH))


def _layer_norm(h, g, b):
    mu = jnp.mean(h, axis=-1, keepdims=True)
    d = h - mu
    var = jnp.mean(d * d, axis=-1, keepdims=True)
    return d * lax.rsqrt(var + LN_EPS) * g + b


def _outproj_ln_kernel(a_ref, w_ref, x_ref, g_ref, b_ref, o_ref, *, precision):
    k = pl.program_id(1)
    part = jnp.dot(a_ref[...], w_ref[...], preferred_element_type=F32, precision=precision)

    @pl.when(k == 0)
    def _():
        o_ref[...] = part

    @pl.when(k > 0)
    def _():
        o_ref[...] += part

    @pl.when(k == pl.num_programs(1) - 1)
    def _():
        o_ref[...] = _layer_norm(DEEPNORM_ALPHA * x_ref[...] + o_ref[...], g_ref[...], b_ref[...])


def _outproj_ln(mix, w_out, x, g, b, tm=256, tk=512, precision=None):
    n = mix.shape[0]
    return pl.pallas_call(
        functools.partial(_outproj_ln_kernel, precision=precision),
        grid=(n // tm, D_MODEL // tk),
        in_specs=[pl.BlockSpec((tm, tk), lambda i, k: (i, k)),
                  pl.BlockSpec((tk, D_MODEL), lambda i, k: (k, 0)),
                  pl.BlockSpec((tm, D_MODEL), lambda i, k: (i, 0)),
                  pl.BlockSpec((1, D_MODEL), lambda i, k: (0, 0)),
                  pl.BlockSpec((1, D_MODEL), lambda i, k: (0, 0))],
        out_specs=pl.BlockSpec((tm, D_MODEL), lambda i, k: (i, 0)),
        out_shape=jax.ShapeDtypeStruct((n, D_MODEL), F32),
        compiler_params=_cp(("parallel", "arbitrary")),
        name=f"outproj_ln_m{n}",
    )(mix, w_out, x, g.reshape(1, D_MODEL), b.reshape(1, D_MODEL))


def _router_kernel(x_ref, w_ref, bias_ref, o_ref):
    tm = x_ref.shape[0]
    logits = _mm_hi(x_ref[...], w_ref[...])
    lt = logits.T[0:N_EXPERTS, :]
    sc = _sigmoid(lt)
    bi = sc + bias_ref[0:N_EXPERTS, :]
    s_row = [sc[e:e + 1, :] for e in range(N_EXPERTS)]
    b_row = [bi[e:e + 1, :] for e in range(N_EXPERTS)]
    gscore = []
    for gi in range(N_EXPERT_GROUPS):
        a, b, c, d = b_row[gi * 4:gi * 4 + 4]
        hi1, lo1, hi2, lo2 = jnp.maximum(a, b), jnp.minimum(a, b), jnp.maximum(c, d), jnp.minimum(c, d)
        top1 = jnp.maximum(hi1, hi2)
        top2 = jnp.maximum(jnp.minimum(hi1, hi2), jnp.maximum(lo1, lo2))
        gscore.append(top1 + top2)
    gates = []
    for gi in range(N_EXPERT_GROUPS):
        best = None
        for gj in range(N_EXPERT_GROUPS):
            if gj == gi:
                continue
            ok = (gscore[gi] > gscore[gj]) if gj < gi else (gscore[gi] >= gscore[gj])
            best = ok if best is None else (best & ok)
        members = range(gi * 4, gi * 4 + 4)
        chosen = []
        for e in members:
            rank = jnp.zeros((1, tm), F32)
            for e2 in members:
                if e2 == e:
                    continue
                ahead = (b_row[e2] > b_row[e]) if e2 > e else (b_row[e2] >= b_row[e])
                rank = rank + ahead.astype(F32)
            chosen.append(best & (rank < 2.0))
        gates.append((members, chosen))
    total = jnp.zeros((1, tm), F32)
    for members, chosen in gates:
        for e, ch in zip(members, chosen):
            total = total + jnp.where(ch, s_row[e], 0.0)
    rowi = lax.broadcasted_iota(I32, (LANES, tm), 0)
    gt = jnp.zeros((LANES, tm), F32)
    for members, chosen in gates:
        for e, ch in zip(members, chosen):
            gt = jnp.where(rowi == e, jnp.where(ch, s_row[e] / total, 0.0), gt)
    o_ref[...] = gt.T


def _router(x, router_w, router_bias):
    n = x.shape[0]
    tm = 256
    wpad = jnp.zeros((D_MODEL, LANES), F32).at[:, :N_EXPERTS].set(router_w)
    bpad = jnp.zeros((LANES, 1), F32).at[:N_EXPERTS, 0].set(router_bias)
    return pl.pallas_call(
        _router_kernel,
        grid=(n // tm,),
        in_specs=[pl.BlockSpec((tm, D_MODEL), lambda i: (i, 0)),
                  pl.BlockSpec((D_MODEL, LANES), lambda i: (0, 0)),
                  pl.BlockSpec((LANES, 1), lambda i: (0, 0))],
        out_specs=pl.BlockSpec((tm, LANES), lambda i: (i, 0)),
        out_shape=jax.ShapeDtypeStruct((n, LANES), F32),
        compiler_params=_cp(("parallel",)),
        name="router",
    )(x, wpad, bpad)


def _moe_dense_kernel(x_ref, gates_ref, wg_ref, wu_ref, wd_ref, g_ref, b_ref, o_ref, xb_ref):
    e = pl.program_id(1)
    f = pl.program_id(2)
    first = (e == 0) & (f == 0)
    last = (e == pl.num_programs(1) - 1) & (f == pl.num_programs(2) - 1)

    @pl.when(first)
    def _():
        xb_ref[...] = x_ref[...].astype(BF16)

    xb = xb_ref[...]
    lane = lax.broadcasted_iota(I32, gates_ref.shape, 1)
    gate = jnp.sum(jnp.where(lane == e, gates_ref[...], 0.0), axis=1, keepdims=True)
    hg = jnp.dot(xb, wg_ref[...], preferred_element_type=F32)
    hu = jnp.dot(xb, wu_ref[...], preferred_element_type=F32)
    hmid = (_silu(hg) * hu * gate).astype(BF16)
    part = jnp.dot(hmid, wd_ref[...], preferred_element_type=F32)

    @pl.when(first)
    def _():
        o_ref[...] = part

    @pl.when(jnp.logical_not(first))
    def _():
        o_ref[...] += part

    @pl.when(last)
    def _():
        o_ref[...] = _layer_norm(DEEPNORM_ALPHA * x_ref[...] + o_ref[...], g_ref[...], b_ref[...])


def _moe_dense(x, gates, wg, wu, wd, g, b, tm=256, tf=256):
    n = x.shape[0]
    nf = D_EXPERT // tf
    return pl.pallas_call(
        _moe_dense_kernel,
        grid=(n // tm, N_EXPERTS, nf),
        in_specs=[pl.BlockSpec((tm, D_MODEL), lambda i, e, f: (i, 0)),
                  pl.BlockSpec((tm, LANES), lambda i, e, f: (i, 0)),
                  pl.BlockSpec((None, D_MODEL, tf), lambda i, e, f: (e, 0, f)),
                  pl.BlockSpec((None, D_MODEL, tf), lambda i, e, f: (e, 0, f)),
                  pl.BlockSpec((None, tf, D_MODEL), lambda i, e, f: (e, f, 0)),
                  pl.BlockSpec((1, D_MODEL), lambda i, e, f: (0, 0)),
                  pl.BlockSpec((1, D_MODEL), lambda i, e, f: (0, 0))],
        out_specs=pl.BlockSpec((tm, D_MODEL), lambda i, e, f: (i, 0)),
        out_shape=jax.ShapeDtypeStruct((n, D_MODEL), F32),
        scratch_shapes=[pltpu.VMEM((tm, D_MODEL), BF16)],
        compiler_params=_cp(("parallel", "arbitrary", "arbitrary")),
        name="moe_dense",
    )(x, gates, wg, wu, wd, g.reshape(1, D_MODEL), b.reshape(1, D_MODEL))


def _split_w_in(w):
    sizes = (A_WIDTH, KV_WIDTH, KV_WIDTH, IDX_HEADS * IDX_DIM, IDX_DIM, IDX_HEADS,
             3 * B_WIDTH, B_WIDTH, B_HEADS, B_HEADS, C_WIDTH)
    offs = np.concatenate([[0], np.cumsum(sizes)])
    aq, ak, av, iq, ik, iw, bqkv, bz, bbeta, ba, cu = [w[:, int(offs[i]):int(offs[i + 1])] for i in range(11)]
    w_r128 = jnp.concatenate([aq, ak], axis=1)
    w_r64 = iq
    w_plain = jnp.concatenate([av, bqkv, bz, cu], axis=1)
    pad = jnp.zeros((w.shape[0], LANES - IDX_DIM - IDX_HEADS - 2 * B_HEADS), w.dtype)
    w_small = jnp.concatenate([ik, iw, bbeta, ba, pad], axis=1)
    return w_r128, w_r64, w_plain, w_small


PL_V, PL_QKV, PL_Z, PL_U = 0, KV_WIDTH, KV_WIDTH + 3 * B_WIDTH, KV_WIDTH + 4 * B_WIDTH


def _layer(l, x, tabs, cache_k, cache_v, cache_idx_k, state_conv, state_delta, state_ssm_re, state_ssm_im,
           page_table, w_in, w_out, conv_w, delta_a_log, delta_dt_bias, delta_norm_g, s5, glu_w, glu_b,
           ln1_g, ln1_b, router_w, router_bias, exp_gate, exp_up, exp_down, ln2_g, ln2_b):
    (t128, s128), (t64, s64), (tsm, ssm) = tabs
    w_f32 = _split_w_in(w_in[l])
    w_r128, w_r64, w_plain, w_small = [w.astype(BF16) for w in w_f32]
    xb = x.astype(BF16)
    qa = _proj(xb, w_r128, t128, s128)
    iq = _proj(xb, w_r64, t64, s64)
    pm = _proj(xb, w_plain)
    sm = _proj(xb, w_small, tsm, ssm, tn=LANES)
    S = slice(NP, NP + DEC_BATCH)
    hi = lax.Precision.HIGHEST
    x_s = x[S]
    rows = lambda tab: tuple(t[S] for t in tab)
    qa_s = _proj(x_s, w_f32[0], rows(t128), s128, tm=DEC_BATCH, precision=hi)
    iq_sf = _proj(x_s, w_f32[1], rows(t64), s64, tm=DEC_BATCH, precision=hi)
    pm_s = _proj(x_s, w_f32[2], tm=DEC_BATCH, precision=hi)
    sm_s = _proj(x_s, w_f32[3], rows(tsm), ssm, tn=LANES, tm=DEC_BATCH, precision=hi)

    o_attn_p = _dsa_prompt(qa, iq, sm, pm)
    iq_s = iq_sf.reshape(DEC_BATCH, IDX_HEADS, IDX_DIM)
    w_s = sm_s[:, SM_IW:SM_IW + IDX_HEADS].reshape(DEC_BATCH, IDX_HEADS, 1)
    scores = _dsa_sample_scores(page_table, iq_s, w_s, cache_idx_k, l)
    ck = cache_k.reshape(cache_k.shape[:3] + (KV_WIDTH,))
    cv = cache_v.reshape(cache_v.shape[:3] + (KV_WIDTH,))
    kn = qa_s[:, A_WIDTH:].reshape(DEC_BATCH, A_KV_HEADS, HEAD_DIM)
    vn = pm_s[:, PL_V:PL_V + KV_WIDTH].reshape(DEC_BATCH, A_KV_HEADS, HEAD_DIM)
    ikn = sm_s[:, SM_IK:SM_IK + IDX_DIM]
    o_attn_s = _dsa_sample_attend(page_table, scores, iq_s, w_s, ikn.reshape(DEC_BATCH, 1, IDX_DIM),
                                  qa_s[:, :A_WIDTH].reshape(DEC_BATCH, A_HEADS, HEAD_DIM), kn, vn, ck, cv, l)
    o_attn_s = o_attn_s.reshape(DEC_BATCH, A_WIDTH)

    o_delta_p, delta_p = _delta_prompt(pm, sm, conv_w[l], delta_a_log[l], delta_dt_bias[l], delta_norm_g[l])
    qkv_s = pm_s[:, PL_QKV:PL_QKV + 3 * B_WIDTH]
    o_delta_s, delta_s = _delta_sample(qkv_s.reshape(DEC_BATCH, 1, 3 * B_WIDTH),
                                       pm_s[:, PL_Z:PL_Z + B_WIDTH].reshape(DEC_BATCH, 1, B_WIDTH),
                                       sm_s.reshape(DEC_BATCH, 1, LANES), state_conv[l], conv_w[l],
                                       delta_a_log[l], delta_dt_bias[l], delta_norm_g[l], state_delta[l])
    o_delta_s = o_delta_s.reshape(DEC_BATCH, B_WIDTH)
    conv_p = pm[:NP, PL_QKV:PL_QKV + 3 * B_WIDTH].reshape(BATCH, SEQ, 3 * B_WIDTH)[:, SEQ - (CONV_W - 1):]
    conv_s = jnp.concatenate([state_conv[l][:, 1:], qkv_s[:, None, :]], axis=1)

    u_p = pm[:NP, PL_U:PL_U + C_WIDTH]
    u_g = u_p.reshape(BATCH, SEQ // S5_CHUNK, S5_CHUNK, C_GROUPS, C_GROUP)
    u_g = jnp.transpose(u_g, (3, 0, 1, 2, 4)).reshape(C_GROUPS, S5_ROWS, S5_CHUNK * C_GROUP).astype(BF16)
    y_g, hfin = _s5_prompt(u_g, *s5['prompt'])
    y_p = jnp.transpose(y_g.reshape(C_GROUPS, BATCH, SEQ // S5_CHUNK, S5_CHUNK, C_GROUP), (1, 2, 3, 0, 4))
    y_p = y_p.reshape(NP, C_WIDTH)
    ssm_re_p = jnp.transpose(hfin[:, :BATCH, :C_STATE], (1, 0, 2))
    ssm_im_p = jnp.transpose(hfin[:, :BATCH, C_STATE:], (1, 0, 2))
    y_s, ssm_re_s, ssm_im_s = _s5_sample(pm_s[:, PL_U:PL_U + C_WIDTH],
                                         state_ssm_re[l].reshape(DEC_BATCH, C_GROUPS * C_STATE),
                                         state_ssm_im[l].reshape(DEC_BATCH, C_GROUPS * C_STATE), s5['sample'])
    o_ssm_p = _glu(y_p, glu_w[l].astype(BF16), glu_b[l])
    o_ssm_s = _glu(y_s, glu_w[l], glu_b[l], tm=DEC_BATCH, precision=hi, out_dtype=F32)

    mix_p = jnp.concatenate([o_attn_p, o_delta_p, o_ssm_p], axis=1)
    mix = jnp.concatenate([mix_p, jnp.zeros((N_ALL - NP, D_MODEL), BF16)], axis=0)
    x1 = _outproj_ln(mix, w_out[l].astype(BF16), x, ln1_g[l], ln1_b[l])
    mix_s = jnp.concatenate([o_attn_s, o_delta_s, o_ssm_s], axis=1)
    x1_s = _outproj_ln(mix_s, w_out[l], x_s, ln1_g[l], ln1_b[l], tm=DEC_BATCH, precision=hi)
    x1 = lax.dynamic_update_slice(x1, x1_s, (NP, 0))
    gates = _router(x1, router_w, router_bias)
    x2 = _moe_dense(x1, gates, exp_gate[l].astype(BF16), exp_up[l].astype(BF16), exp_down[l].astype(BF16),
                    ln2_g[l], ln2_b[l])

    k_p = qa[:NP, A_WIDTH:].reshape(BATCH, SEQ, A_KV_HEADS, HEAD_DIM)
    v_p = pm[:NP, PL_V:PL_V + KV_WIDTH].reshape(BATCH, SEQ, A_KV_HEADS, HEAD_DIM)
    ik_p = sm[:NP, SM_IK:SM_IK + IDX_DIM].reshape(BATCH, SEQ, IDX_DIM)
    st_p = (k_p, v_p, ik_p, conv_p, delta_p, ssm_re_p, ssm_im_p)
    st_s = (kn.reshape(DEC_BATCH, 1, A_KV_HEADS, HEAD_DIM), vn.reshape(DEC_BATCH, 1, A_KV_HEADS, HEAD_DIM),
            ikn.reshape(DEC_BATCH, 1, IDX_DIM), conv_s, delta_s,
            ssm_re_s.reshape(DEC_BATCH, C_GROUPS, C_STATE), ssm_im_s.reshape(DEC_BATCH, C_GROUPS, C_STATE))
    return x2, st_p, st_s


def kernel(x_prompt, x_sample, cache_k, cache_v, cache_idx_k, state_conv, state_delta, state_ssm_re, state_ssm_im, page_table, w_in, w_out, conv_w, delta_a_log, delta_dt_bias, delta_norm_g, ssm_a_re, ssm_a_im, ssm_log_dt, ssm_b_re, ssm_b_im, ssm_c_re, ssm_c_im, ssm_d, glu_w, glu_b, ln1_g, ln1_b, router_w, router_bias, exp_gate, exp_up, exp_down, ln2_g, ln2_b):
    assert x_prompt.shape == (BATCH, SEQ, D_MODEL) and x_sample.shape == (DEC_BATCH, 1, D_MODEL)
    tabs = _rope_tables()
    x = jnp.concatenate([x_prompt.reshape(NP, D_MODEL), x_sample.reshape(DEC_BATCH, D_MODEL),
                         jnp.zeros((N_ALL - NP - DEC_BATCH, D_MODEL), F32)], axis=0)
    st_p, st_s = [], []
    for l in range(DEPTH):
        sp = (ssm_a_re[l], ssm_a_im[l], ssm_log_dt[l], ssm_b_re[l], ssm_b_im[l], ssm_c_re[l], ssm_c_im[l], ssm_d[l])
        s5 = {'prompt': _s5_prompt_operands(*sp), 'sample': _s5_sample_operands(*sp)}
        x, sp_l, ss_l = _layer(l, x, tabs, cache_k, cache_v, cache_idx_k, state_conv, state_delta, state_ssm_re,
                               state_ssm_im, page_table, w_in, w_out, conv_w, delta_a_log, delta_dt_bias,
                               delta_norm_g, s5, glu_w, glu_b, ln1_g, ln1_b, router_w, router_bias,
                               exp_gate, exp_up, exp_down, ln2_g, ln2_b)
        st_p.append(sp_l)
        st_s.append(ss_l)
    y_prompt = x[:NP].reshape(BATCH, SEQ, D_MODEL)
    y_sample = x[NP:NP + DEC_BATCH].reshape(DEC_BATCH, 1, D_MODEL)
    stack = lambda sts, i: jnp.stack([s[i] for s in sts])
    return ((y_prompt, y_sample) + tuple(stack(st_p, i) for i in range(7)) + tuple(stack(st_s, i) for i in range(7)))
```
